```python
import jax, jax.numpy as jnp
from jax import lax
import numpy as np


D_MODEL = 1024
BATCH = 8
SEQ = 2048
DEPTH = 2

GRID_W = 64
CTX_LEN = 256
HEAD_DIM = 64
MIX_W = D_MODEL
POOL_W = MIX_W // 4
POOL_WINDOWS = (2, 4, 8, 16)
POOL_GW = POOL_W // len(POOL_WINDOWS)
ATT_W = MIX_W // 2
ATT_HEADS = ATT_W // HEAD_DIM
ATT_KV_HEADS = max(1, ATT_HEADS // 4)
ATT_GROUP = ATT_HEADS // ATT_KV_HEADS
ATT_KV_W = ATT_KV_HEADS * HEAD_DIM
WINDOW = 128
ATT_BLOCK = 128
HG_VW = MIX_W - POOL_W - ATT_W
HG_HEADS = HG_VW // HEAD_DIM
HG_DK = HEAD_DIM
HG_DV = HEAD_DIM
HG_KW = HG_HEADS * HG_DK
HG_CHUNK = 64
IN_SIZES = (POOL_W, ATT_W, ATT_KV_W, ATT_KV_W, HG_KW, HG_KW, HG_KW, HG_VW, HG_VW)
IN_W = sum(IN_SIZES)
D_FF = 4 * D_MODEL
ROPE_BASE = 10000.0
EPS = 1e-6
NEG_BIG = -1e30

kernel_name = "hybrid_pool_swa_hgrn2_prefix_dit"


def rmsnorm(x, w):
    xf = x.astype(jnp.float32)
    y = xf * lax.rsqrt(jnp.mean(xf * xf, axis=-1, keepdims=True) + EPS)
    return (y * w.astype(jnp.float32)).astype(x.dtype)


def axial_rope_tables(n):
    rows_n = n // GRID_W
    row = jnp.repeat(jnp.arange(rows_n), GRID_W).astype(jnp.float32)
    col = jnp.tile(jnp.arange(GRID_W), rows_n).astype(jnp.float32)
    axis_dims = HEAD_DIM // 2
    inv = ROPE_BASE ** (-jnp.arange(0, axis_dims, 2, dtype=jnp.float32) / axis_dims)
    ar = row[:, None] * inv
    ac = col[:, None] * inv
    return (jnp.cos(ar), jnp.sin(ar), jnp.cos(ac), jnp.sin(ac))


def _rot_half(x, cos, sin):
    x1, x2 = jnp.split(x, 2, axis=-1)
    c = cos[:, None, :]
    s = sin[:, None, :]
    return jnp.concatenate([x1 * c - x2 * s, x2 * c + x1 * s], axis=-1)


def apply_axial_rope(x, tabs):
    cr, sr, cc, sc = tabs
    xr, xc = jnp.split(x, 2, axis=-1)
    return jnp.concatenate([_rot_half(xr, cr, sr), _rot_half(xc, cc, sc)], axis=-1).astype(x.dtype)


def split_in(z):
    idx = np.cumsum(IN_SIZES)[:-1].tolist()
    return jnp.split(z, idx, axis=-1)


def pool_mixer(u, pool_w, pool_scale):
    B, n, _ = u.shape
    uf = u.astype(jnp.float32)
    csum = jnp.concatenate([jnp.zeros_like(uf[:, :1]), jnp.cumsum(uf, axis=1)], axis=1)
    pos = jnp.arange(n)
    outs = []
    for g, w in enumerate(POOL_WINDOWS):
        lo = jnp.clip(pos - w // 2, 0, n)
        hi = jnp.clip(pos + w // 2, 0, n)
        cg = csum[..., g * POOL_GW:(g + 1) * POOL_GW]
        cnt = (hi - lo).astype(jnp.float32)[None, :, None]
        outs.append((cg[:, hi] - cg[:, lo]) / cnt)
    d = jnp.concatenate(outs, axis=-1) - uf
    d = d.reshape(B, n, len(POOL_WINDOWS), POOL_GW)
    y = jnp.einsum('bngc,gcd->bngd', d, pool_w.astype(jnp.float32)).reshape(B, n, POOL_W)
    return (y * pool_scale.astype(jnp.float32)).astype(u.dtype)


def softmax_with_sink(s, sink):
    sk = jnp.broadcast_to(sink.astype(jnp.float32)[:, :, None, None], s.shape[:-1] + (1,))
    p = jax.nn.softmax(jnp.concatenate([s, sk], axis=-1), axis=-1)
    return p[..., :-1]


def window_attention(q, k, v, kc, vc, sink):
    B, S = q.shape[:2]
    L = kc.shape[1]
    nb = S // ATT_BLOCK
    qb = q.reshape(B, nb, ATT_BLOCK, ATT_KV_HEADS, ATT_GROUP, HEAD_DIM)
    pad = ((0, 0), (ATT_BLOCK, ATT_BLOCK), (0, 0), (0, 0))
    kp = jnp.pad(k, pad)
    vp = jnp.pad(v, pad)

    def bands(a):
        return jnp.concatenate([a[:, i * ATT_BLOCK:i * ATT_BLOCK + S].reshape(B, nb, ATT_BLOCK, ATT_KV_HEADS, HEAD_DIM)
                                for i in range(3)], axis=2)

    kb, vb = bands(kp), bands(vp)
    scale = HEAD_DIM ** -0.5
    s_loc = jnp.einsum('bnqkgd,bnskd->bnkgqs', qb, kb).astype(jnp.float32) * scale
    s_ctx = jnp.einsum('bnqkgd,bskd->bnkgqs', qb, kc).astype(jnp.float32) * scale
    qpos = jnp.arange(S).reshape(nb, ATT_BLOCK)
    kpos = jnp.arange(nb)[:, None] * ATT_BLOCK - ATT_BLOCK + jnp.arange(3 * ATT_BLOCK)[None, :]
    valid = ((kpos[:, None, :] >= 0) & (kpos[:, None, :] < S)
             & (jnp.abs(qpos[:, :, None] - kpos[:, None, :]) <= WINDOW))
    s_loc = jnp.where(valid[None, :, None, None], s_loc, NEG_BIG)
    p = softmax_with_sink(jnp.concatenate([s_loc, s_ctx], axis=-1), sink.reshape(ATT_KV_HEADS, ATT_GROUP))
    nk = 3 * ATT_BLOCK
    p_loc = p[..., :nk].astype(v.dtype)
    p_ctx = p[..., nk:nk + L].astype(v.dtype)
    o = (jnp.einsum('bnkgqs,bnskd->bnqkgd', p_loc, vb)
         + jnp.einsum('bnkgqs,bskd->bnqkgd', p_ctx, vc))
    return o.reshape(B, S, ATT_W)


def context_attention(qc, kc, vc, sink):
    B, L = qc.shape[:2]
    qg = qc.reshape(B, L, ATT_KV_HEADS, ATT_GROUP, HEAD_DIM)
    s = jnp.einsum('blkgd,bskd->bkgls', qg, kc).astype(jnp.float32) * (HEAD_DIM ** -0.5)
    p = softmax_with_sink(s, sink.reshape(ATT_KV_HEADS, ATT_GROUP)).astype(vc.dtype)
    return jnp.einsum('bkgls,bskd->blkgd', p, vc).reshape(B, L, ATT_W)


def gla_chunk_scan(q, k, v, logf, s0):
    B, n, H, _ = q.shape
    nc = n // HG_CHUNK

    def to_chunks(a):
        return jnp.moveaxis(a.reshape(B, nc, HG_CHUNK, H, a.shape[-1]), 1, 0)

    tri = jnp.tril(jnp.ones((HG_CHUNK, HG_CHUNK), dtype=bool))[None, :, :, None, None]

    def step(S, inp):
        qc, kc, vc, gc = inp
        b = jnp.cumsum(gc, axis=1)
        diff = jnp.where(tri, b[:, :, None] - b[:, None, :], NEG_BIG)
        a = jnp.sum(qc[:, :, None] * kc[:, None] * jnp.exp(diff), axis=-1)
        o = jnp.einsum('btsh,bshv->bthv', a, vc) + jnp.einsum('bthk,bhkv->bthv', qc * jnp.exp(b), S)
        b_end = b[:, -1]
        S_new = (jnp.exp(b_end)[..., None] * S
                 + jnp.einsum('bshk,bshv->bhkv', kc * jnp.exp(b_end[:, None] - b), vc))
        return S_new, o

    s_fin, o = lax.scan(step, s0, (to_chunks(q), to_chunks(k), to_chunks(v), to_chunks(logf)))
    return jnp.moveaxis(o, 0, 1).reshape(B, n, H, v.shape[-1]), s_fin


def _heads(a):
    B, n = a.shape[:2]
    return a.astype(jnp.float32).reshape(B, n, HG_HEADS, -1)


def hgrn2_gates(f_pre, lb):
    lb = lb.astype(jnp.float32)
    f = lb + (1.0 - lb) * jax.nn.sigmoid(f_pre.astype(jnp.float32))
    return _heads(1.0 - f), _heads(jnp.log(f))


def hgrn2_mixer(q, ff, fb, i, g, cq, cff, cfb, ci, cg, lb_f, lb_b, norm_w, need_ctx):
    B = q.shape[0]
    s0 = jnp.zeros((B, HG_HEADS, HG_DK, HG_DV), jnp.float32)
    flip = lambda a: jnp.flip(a, axis=1)
    qh, vh = _heads(jax.nn.silu(q)), _heads(i)
    cqh, cvh = _heads(jax.nn.silu(cq)), _heads(ci)
    kf, gf = hgrn2_gates(ff, lb_f)
    ckf, cgf = hgrn2_gates(cff, lb_f)
    oc_f, sc_f = gla_chunk_scan(cqh, ckf, cvh, cgf, s0)
    o_f, _ = gla_chunk_scan(qh, kf, vh, gf, sc_f)
    kb, gb = hgrn2_gates(fb, lb_b)
    ckb, cgb = hgrn2_gates(cfb, lb_b)
    oc_b, sc_b = gla_chunk_scan(flip(cqh), flip(ckb), flip(cvh), flip(cgb), s0)
    o_b, _ = gla_chunk_scan(flip(qh), flip(kb), flip(vh), flip(gb), sc_b)

    def readout(o, gate):
        Bn, n = o.shape[:2]
        return (rmsnorm(o, norm_w).reshape(Bn, n, HG_VW) * jax.nn.silu(gate.astype(jnp.float32))).astype(gate.dtype)

    y = readout(o_f + flip(o_b), g)
    yc = readout(oc_f + flip(oc_b), cg) if need_ctx else None
    return y, yc


def token_mixer(h, hc, w_in, pool_w, pool_scale, sink, lb_f, lb_b, hg_norm_w, w_out, rope, need_ctx):
    B, S, _ = h.shape
    L = hc.shape[1]
    pa, qa, ka, va, qh, ff, fb, ih, gh = split_in(h @ w_in)
    cpa, cqa, cka, cva, cqh, cff, cfb, cih, cgh = split_in(hc @ w_in)
    ya = pool_mixer(pa, pool_w, pool_scale)
    q = apply_axial_rope(qa.reshape(B, S, ATT_HEADS, HEAD_DIM), rope)
    k = apply_axial_rope(ka.reshape(B, S, ATT_KV_HEADS, HEAD_DIM), rope)
    v = va.reshape(B, S, ATT_KV_HEADS, HEAD_DIM)
    kc = cka.reshape(B, L, ATT_KV_HEADS, HEAD_DIM)
    vc = cva.reshape(B, L, ATT_KV_HEADS, HEAD_DIM)
    yb = window_attention(q, k, v, kc, vc, sink)
    yc, ycc = hgrn2_mixer(qh, ff, fb, ih, gh, cqh, cff, cfb, cih, cgh, lb_f, lb_b, hg_norm_w, need_ctx)
    y = jnp.concatenate([ya, yb, yc], axis=-1) @ w_out
    if not need_ctx:
        return y, None
    yca = pool_mixer(cpa, pool_w, pool_scale)
    ycb = context_attention(cqa.reshape(B, L, ATT_HEADS, HEAD_DIM), kc, vc, sink)
    y_ctx = jnp.concatenate([yca, ycb, ycc], axis=-1) @ w_out
    return y, y_ctx


def sqrelu_mlp(h, w1, w2):
    return jnp.square(jax.nn.relu(h @ w1)) @ w2


def setup_inputs(seed: int = 0) -> dict:
    key = jax.random.key(seed)
    ks = jax.random.split(key, 20)
    nrm = lambda k, shape, s: jax.random.normal(k, shape, jnp.float32) * s
    return {
        'x': nrm(ks[0], (BATCH, SEQ, D_MODEL), 1.0),
        'c': nrm(ks[1], (BATCH, D_MODEL), 1.0),
        'ctx': nrm(ks[2], (BATCH, CTX_LEN, D_MODEL), 1.0),
        'c_ctx': nrm(ks[3], (D_MODEL,), 1.0),
        'w_ada': nrm(ks[4], (DEPTH, D_MODEL, 6 * D_MODEL), D_MODEL ** -0.5),
        'b_ada': nrm(ks[5], (DEPTH, 6 * D_MODEL), 0.02),
        'norm1_w': 1.0 + nrm(ks[6], (DEPTH, D_MODEL), 0.1),
        'w_in': nrm(ks[7], (DEPTH, D_MODEL, IN_W), D_MODEL ** -0.5),
        'pool_w': nrm(ks[8], (DEPTH, len(POOL_WINDOWS), POOL_GW, POOL_GW), POOL_GW ** -0.5),
        'pool_scale': 1.0 + nrm(ks[9], (DEPTH, POOL_W), 0.1),
        'attn_sink': nrm(ks[10], (DEPTH, ATT_HEADS), 0.5),
        'hg_lower': nrm(ks[11], (DEPTH, 2, HG_KW), 0.5),
        'hg_norm_w': 1.0 + nrm(ks[12], (DEPTH, HG_DV), 0.1),
        'w_out': nrm(ks[13], (DEPTH, MIX_W, D_MODEL), MIX_W ** -0.5),
        'norm2_w': 1.0 + nrm(ks[14], (DEPTH, D_MODEL), 0.1),
        'w_mlp1': nrm(ks[15], (DEPTH, D_MODEL, D_FF), D_MODEL ** -0.5),
        'w_mlp2': nrm(ks[16], (DEPTH, D_FF, D_MODEL), D_FF ** -0.5),
        'final_norm_w': 1.0 + nrm(ks[17], (D_MODEL,), 0.1),
    }


def reference(x, c, ctx, c_ctx, w_ada, b_ada, norm1_w, w_in, pool_w, pool_scale, attn_sink,
              hg_lower, hg_norm_w, w_out, norm2_w, w_mlp1, w_mlp2, final_norm_w):
    rope = axial_rope_tables(x.shape[1])
    lbs = jax.nn.softmax(hg_lower.astype(jnp.float32), axis=0)
    lbs = jnp.cumsum(lbs, axis=0) - lbs[0]
    sc_lat = jax.nn.silu(c)
    sc_ctx = jax.nn.silu(c_ctx)
    xc = ctx
    for l in range(DEPTH):
        need_ctx = l < DEPTH - 1
        m = (sc_lat @ w_ada[l] + b_ada[l])[:, None, :]
        mc = sc_ctx @ w_ada[l] + b_ada[l]
        sh1, s1, g1, sh2, s2, g2 = jnp.split(m, 6, axis=-1)
        csh1, cs1, cg1, csh2, cs2, cg2 = jnp.split(mc, 6, axis=-1)
        h = rmsnorm(x, norm1_w[l]) * (1.0 + s1) + sh1
        hc = rmsnorm(xc, norm1_w[l]) * (1.0 + cs1) + csh1
        y, y_ctx = token_mixer(h, hc, w_in[l], pool_w[l], pool_scale[l], attn_sink[l],
                               lbs[l, 0], lbs[l, 1], hg_norm_w[l], w_out[l], rope, need_ctx)
        x = x + g1 * y
        h = rmsnorm(x, norm2_w[l]) * (1.0 + s2) + sh2
        x = x + g2 * sqrelu_mlp(h, w_mlp1[l], w_mlp2[l])
        if need_ctx:
            xc = xc + cg1 * y_ctx
            hc = rmsnorm(xc, norm2_w[l]) * (1.0 + cs2) + csh2
            xc = xc + cg2 * sqrelu_mlp(hc, w_mlp1[l], w_mlp2[l])
    return rmsnorm(x, final_norm_w)
```

```python
import functools

import jax
import jax.numpy as jnp
import numpy as np
from jax import lax
from jax.experimental import pallas as pl
from jax.experimental.pallas import tpu as pltpu

F32 = jnp.float32
BF16 = jnp.bfloat16

HEAD_DIM = 64
GRID_W = 64
POOL_WINDOWS = (2, 4, 8, 16)
WINDOW = 128
ATT_BLOCK = 128
ATT_GROUP = 4
HG_CHUNK = 64
HG_SUB = 16
ROPE_BASE = 10000.0
EPS = 1e-6
NEG_BIG = -1e30

V7X_LANES = 128
V7X_VMEM_LIMIT_BYTES = 56 * 1024 * 1024


def _cparams(semantics):
    return pltpu.CompilerParams(dimension_semantics=semantics, vmem_limit_bytes=V7X_VMEM_LIMIT_BYTES)


def _dot(a, b):
    return jnp.dot(a, b, preferred_element_type=F32)


def _dot_nt(a, b):
    return lax.dot_general(a, b, (((1,), (1,)), ((), ())), preferred_element_type=F32)


def _silu(x):
    return x / (1.0 + jnp.exp(-x))


def _rms(x):
    return x * lax.rsqrt(jnp.mean(x * x, axis=-1, keepdims=True) + EPS)


def _mod_kernel(c_ref, w_ref, b_ref, o_ref):
    s = _silu(c_ref[...])
    o_ref[0] = jnp.dot(s, w_ref[0], preferred_element_type=F32, precision=lax.Precision.HIGHEST) + b_ref[0]


def _modulation(cvec, w_ada, b_ada, tn=1024):
    depth, d, n6 = w_ada.shape
    rows = cvec.shape[0]
    return pl.pallas_call(
        _mod_kernel,
        grid=(depth, n6 // tn),
        in_specs=[
            pl.BlockSpec((rows, d), lambda l, j: (0, 0)),
            pl.BlockSpec((1, d, tn), lambda l, j: (l, 0, j)),
            pl.BlockSpec((1, 1, tn), lambda l, j: (l, 0, j)),
        ],
        out_specs=pl.BlockSpec((1, rows, tn), lambda l, j: (l, 0, j)),
        out_shape=jax.ShapeDtypeStruct((depth, rows, n6), F32),
        compiler_params=_cparams(("arbitrary", "arbitrary")),
        name="modulation",
    )(cvec, w_ada, b_ada.reshape(depth, 1, n6))


def _swap16(x, lane):
    up = pltpu.roll(x, V7X_LANES - 16, 1)
    dn = pltpu.roll(x, 16, 1)
    return jnp.where((lane & 16) == 0, up, dn)


def _inproj_kernel(*refs, layer, rope, sizes):
    if rope:
        x_ref, nw_ref, sh_ref, sc_ref, w_ref, hl_ref, cos_ref, sin_ref = refs[:8]
        outs = refs[8:]
    else:
        x_ref, nw_ref, sh_ref, sc_ref, w_ref, hl_ref = refs[:6]
        outs = refs[6:]
    pool_ref, q_ref, k_ref, v_ref, hq_ref, kf_ref, gf_ref, kb_ref, gb_ref, iv_ref, sg_ref = outs
    pool_w, att_w, kv_w, hk_w, hv_w = sizes

    h = (_rms(x_ref[...]) * nw_ref[...]) * (1.0 + sc_ref[0]) + sh_ref[0]
    z = _dot(h.astype(BF16), w_ref[...])

    off = [0]

    def take(width):
        a = z[:, off[0]:off[0] + width]
        off[0] += width
        return a

    pool_ref[...] = take(pool_w)
    qa, ka, va = take(att_w), take(kv_w), take(kv_w)
    if rope:
        cos, sin = cos_ref[...], sin_ref[...]
        lane = lax.broadcasted_iota(jnp.int32, cos.shape, 1)

        def rot(a):
            parts = []
            for j in range(a.shape[1] // V7X_LANES):
                blk = a[:, j * V7X_LANES:(j + 1) * V7X_LANES]
                parts.append(blk * cos + _swap16(blk, lane) * sin)
            return parts[0] if len(parts) == 1 else jnp.concatenate(parts, axis=1)

        qa, ka = rot(qa), rot(ka)
    q_ref[...] = qa.astype(BF16)
    k_ref[...] = ka.astype(BF16)
    v_ref[...] = va.astype(BF16)
    hq_ref[...] = _silu(take(hk_w))

    hl = hl_ref[...]
    ex = jnp.exp(hl - jnp.max(hl, axis=0, keepdims=True))
    sm = ex / jnp.sum(ex, axis=0, keepdims=True)
    lb = jnp.zeros(sm.shape[1:], F32)
    for j in range(1, layer + 1):
        lb = lb + sm[j]

    def gates(pre, lbd):
        f = lbd + (1.0 - lbd) * (1.0 / (1.0 + jnp.exp(-pre)))
        return 1.0 - f, jnp.log(f)

    kf_ref[...], gf_ref[...] = gates(take(hk_w), lb[0:1])
    kb_ref[...], gb_ref[...] = gates(take(hk_w), lb[1:2])
    iv_ref[...] = take(hv_w)
    sg_ref[...] = _silu(take(hv_w))


def _inproj(x, mod3, mod_row_of_tile, nw, w_in_bf, hg_lower, rope_tabs, sizes, layer, tm):
    rows, d = x.shape
    n_in = w_in_bf.shape[1]
    pool_w, att_w, kv_w, hk_w, hv_w = sizes
    depth = hg_lower.shape[0]
    rope = rope_tabs is not None
    in_specs = [
        pl.BlockSpec((tm, d), lambda i: (i, 0)),
        pl.BlockSpec((1, d), lambda i: (0, 0)),
        pl.BlockSpec((1, 1, d), lambda i: (mod_row_of_tile(i) * 6 + 0, 0, 0)),
        pl.BlockSpec((1, 1, d), lambda i: (mod_row_of_tile(i) * 6 + 1, 0, 0)),
        pl.BlockSpec((d, n_in), lambda i: (0, 0)),
        pl.BlockSpec((depth, 2, hk_w), lambda i: (0, 0, 0)),
    ]
    args = [x, nw.reshape(1, d), mod3, mod3, w_in_bf, hg_lower]
    if rope:
        cos, sin = rope_tabs
        tiles_per_seq = cos.shape[0] // tm
        in_specs += [pl.BlockSpec((tm, V7X_LANES), lambda i: (i % tiles_per_seq, 0))] * 2
        args += [cos, sin]
    widths = [(pool_w, F32), (att_w, BF16), (kv_w, BF16), (kv_w, BF16), (hk_w, F32), (hk_w, F32), (hk_w, F32),
              (hk_w, F32), (hk_w, F32), (hv_w, F32), (hv_w, F32)]
    return pl.pallas_call(
        functools.partial(_inproj_kernel, layer=layer, rope=rope, sizes=sizes),
        grid=(rows // tm,),
        in_specs=in_specs,
        out_specs=[pl.BlockSpec((tm, w), lambda i: (i, 0)) for w, _ in widths],
        out_shape=[jax.ShapeDtypeStruct((rows, w), dt) for w, dt in widths],
        compiler_params=_cparams(("arbitrary",)),
        name="inproj_rope" if rope else "inproj_ctx",
    )(*args)


def _pool_kernel(u_ref, pw_ref, ps_ref, o_ref, *, n, group_w):
    u = u_ref[...]
    t = lax.broadcasted_iota(jnp.int32, u.shape, 0)
    grp = lax.broadcasted_iota(jnp.int32, u.shape, 1) // group_w

    def down(x, m):
        return jnp.where(t >= m, pltpu.roll(x, m, 0), 0.0)

    def up(x, m):
        return jnp.where(t < n - m, pltpu.roll(x, n - m, 0), 0.0)

    trail, lead, m = u, u, 1
    win = None
    half = None
    for g, w in enumerate(POOL_WINDOWS):
        while m < w // 2:
            trail = trail + down(trail, m)
            lead = lead + up(lead, m)
            m *= 2
        s = down(trail, 1) + lead
        win = s if win is None else jnp.where(grp == g, s, win)
        half = jnp.full(u.shape, w // 2, jnp.int32) if half is None else jnp.where(grp == g, w // 2, half)
    cnt = (jnp.minimum(t + half, n) - jnp.maximum(t - half, 0)).astype(F32)
    d = win / cnt - u
    o_ref[...] = (_dot(d.astype(BF16), pw_ref[...]) * ps_ref[...]).astype(BF16)


def _pool_mixer(u, pw_bd, ps, n):
    rows, c = u.shape
    return pl.pallas_call(
        functools.partial(_pool_kernel, n=n, group_w=c // len(POOL_WINDOWS)),
        grid=(rows // n,),
        in_specs=[
            pl.BlockSpec((n, c), lambda b: (b, 0)),
            pl.BlockSpec((c, c), lambda b: (0, 0)),
            pl.BlockSpec((1, c), lambda b: (0, 0)),
        ],
        out_specs=pl.BlockSpec((n, c), lambda b: (b, 0)),
        out_shape=jax.ShapeDtypeStruct((rows, c), BF16),
        compiler_params=_cparams(("arbitrary",)),
        name="pool_mixer",
    )(u, pw_bd, ps.reshape(1, c))


def _softmax_heads(q, k_parts, v_parts, masks, sink_ref, o_ref, kv_heads):
    scale = HEAD_DIM ** -0.5
    outs = []
    for kh in range(kv_heads):
        ks = [kp[:, kh * HEAD_DIM:(kh + 1) * HEAD_DIM] for kp in k_parts]
        vs = [vp[:, kh * HEAD_DIM:(kh + 1) * HEAD_DIM] for vp in v_parts]
        for g in range(ATT_GROUP):
            hh = kh * ATT_GROUP + g
            qh = q[:, hh * HEAD_DIM:(hh + 1) * HEAD_DIM]
            sink = sink_ref[hh:hh + 1, 0:1]
            ss = []
            for kpart, msk in zip(ks, masks):
                s = _dot_nt(qh, kpart) * scale
                ss.append(s if msk is None else jnp.where(msk, s, NEG_BIG))
            m = sink
            for s in ss:
                m = jnp.maximum(m, jnp.max(s, axis=-1, keepdims=True))
            den = jnp.exp(sink - m)
            acc = None
            for s, vpart in zip(ss, vs):
                p = jnp.exp(s - m)
                den = den + jnp.sum(p, axis=-1, keepdims=True)
                pv = _dot(p.astype(BF16), vpart)
                acc = pv if acc is None else acc + pv
            outs.append(acc / den)
    o_ref[...] = jnp.concatenate(outs, axis=1).astype(o_ref.dtype)


def _attn_kernel(q_ref, kp_ref, kc_ref, kn_ref, vp_ref, vc_ref, vn_ref, kx_ref, vx_ref, sink_ref, o_ref, *,
                 seq_len, kv_heads):
    qb = pl.program_id(1)
    k_loc = jnp.concatenate([kp_ref[...], kc_ref[...], kn_ref[...]], axis=0)
    v_loc = jnp.concatenate([vp_ref[...], vc_ref[...], vn_ref[...]], axis=0)
    shape = (ATT_BLOCK, 3 * ATT_BLOCK)
    row = lax.broadcasted_iota(jnp.int32, shape, 0)
    col = lax.broadcasted_iota(jnp.int32, shape, 1)
    kpos = (qb - 1) * ATT_BLOCK + col
    rel = row + ATT_BLOCK - col
    valid = (kpos >= 0) & (kpos < seq_len) & (rel <= WINDOW) & (rel >= -WINDOW)
    _softmax_heads(q_ref[...], [k_loc, kx_ref[...]], [v_loc, vx_ref[...]], [valid, None], sink_ref, o_ref, kv_heads)


def _window_attention(q, k, v, kx, vx, sink_b, batch, seq_len, ctx_len):
    rows, att_w = q.shape
    kv_w = k.shape[1]
    nb = seq_len // ATT_BLOCK
    heads = sink_b.shape[0]

    def band(shift):
        return pl.BlockSpec((ATT_BLOCK, kv_w), lambda b, i: (b * nb + jnp.clip(i + shift, 0, nb - 1), 0))

    ctx_spec = pl.BlockSpec((ctx_len, kv_w), lambda b, i: (b, 0))
    return pl.pallas_call(
        functools.partial(_attn_kernel, seq_len=seq_len, kv_heads=kv_w // HEAD_DIM),
        grid=(batch, nb),
        in_specs=[pl.BlockSpec((ATT_BLOCK, att_w), lambda b, i: (b * nb + i, 0)),
                  band(-1), band(0), band(1), band(-1), band(0), band(1), ctx_spec, ctx_spec,
                  pl.BlockSpec((heads, V7X_LANES), lambda b, i: (0, 0))],
        out_specs=pl.BlockSpec((ATT_BLOCK, att_w), lambda b, i: (b * nb + i, 0)),
        out_shape=jax.ShapeDtypeStruct((rows, att_w), BF16),
        compiler_params=_cparams(("arbitrary", "arbitrary")),
        name="window_attention",
    )(q, k, k, k, v, v, v, kx, vx, sink_b)


def _ctx_attn_kernel(q_ref, kx_ref, vx_ref, sink_ref, o_ref, *, kv_heads):
    _softmax_heads(q_ref[...], [kx_ref[...]], [vx_ref[...]], [None], sink_ref, o_ref, kv_heads)


def _context_attention(q, kx, vx, sink_b, batch, ctx_len):
    rows, att_w = q.shape
    kv_w = kx.shape[1]
    heads = sink_b.shape[0]
    return pl.pallas_call(
        functools.partial(_ctx_attn_kernel, kv_heads=kv_w // HEAD_DIM),
        grid=(batch,),
        in_specs=[pl.BlockSpec((ctx_len, att_w), lambda b: (b, 0)),
                  pl.BlockSpec((ctx_len, kv_w), lambda b: (b, 0)),
                  pl.BlockSpec((ctx_len, kv_w), lambda b: (b, 0)),
                  pl.BlockSpec((heads, V7X_LANES), lambda b: (0, 0))],
        out_specs=pl.BlockSpec((ctx_len, att_w), lambda b: (b, 0)),
        out_shape=jax.ShapeDtypeStruct((rows, att_w), BF16),
        compiler_params=_cparams(("arbitrary",)),
        name="context_attention",
    )(q, kx, vx, sink_b)


def _hg_chunk(q, k, v, g, st, b_scr, k_src, v_src, reverse):
    c, w = q.shape
    nsub = c // HG_SUB
    t = lax.broadcasted_iota(jnp.int32, (c, w), 0)

    b = g
    m = 1
    while m < c:
        if reverse:
            b = b + jnp.where(t < c - m, pltpu.roll(b, c - m, 0), 0.0)
        else:
            b = b + jnp.where(t >= m, pltpu.roll(b, m, 0), 0.0)
        m *= 2
    b_scr[...] = b
    last = 0 if reverse else c - 1
    b_end = b[last:last + 1, :]

    lane = lax.broadcasted_iota(jnp.int32, (1, w), 1) // HEAD_DIM
    head_masks = [(lane == hh).astype(F32) for hh in range(w // HEAD_DIM)]

    def head_tile(a):
        return jnp.concatenate([a * hm for hm in head_masks], axis=0)

    o = _dot_nt((q * jnp.exp(b)).astype(BF16), st.astype(BF16))
    o_blk = [o[i * HG_SUB:(i + 1) * HG_SUB] for i in range(nsub)]

    for j in range(nsub):
        later = list(range(0, j)) if reverse else list(range(j + 1, nsub))
        if not later:
            continue
        lo, hi = later[0] * HG_SUB, (later[-1] + 1) * HG_SUB
        src = slice(j * HG_SUB, (j + 1) * HG_SUB)
        e_row = j * HG_SUB if reverse else (j + 1) * HG_SUB - 1
        e_j = b[e_row:e_row + 1, :]
        ks = k[src] * jnp.exp(e_j - b[src])
        qt = q[lo:hi] * jnp.exp(b[lo:hi] - e_j)
        a = _dot_nt(qt.astype(BF16), head_tile(ks).astype(BF16))
        upd = _dot(a.astype(BF16), head_tile(v[src]).astype(BF16))
        for n_i, i in enumerate(later):
            o_blk[i] = o_blk[i] + upd[n_i * HG_SUB:(n_i + 1) * HG_SUB]
    o = jnp.concatenate(o_blk, axis=0)

    r_idx = lax.broadcasted_iota(jnp.int32, (w, w), 0) // HEAD_DIM
    c_idx = lax.broadcasted_iota(jnp.int32, (w, w), 1) // HEAD_DIM
    bd = r_idx == c_idx
    ones_bd = jnp.where(bd, 1.0, 0.0).astype(BF16)
    t_loc = t % HG_SUB

    def rows_bcast(load_row, sl):
        return jnp.concatenate(
            [jnp.broadcast_to(load_row(i * HG_SUB + sl), (HG_SUB, w)) for i in range(nsub)], axis=0)

    for sl in range(HG_SUB):
        bs = rows_bcast(lambda r: b_scr[r:r + 1, :], sl)
        ks = rows_bcast(k_src, sl)
        vs = rows_bcast(v_src, sl)
        valid = (t_loc <= sl) if reverse else (t_loc >= sl)
        mm = q * jnp.exp(jnp.where(valid, b - bs, NEG_BIG)) * ks
        o = o + _dot(mm.astype(BF16), ones_bd) * vs

    ke = k * jnp.exp(b_end - b)
    u = _dot(v.T.astype(BF16), ke.astype(BF16))
    st_new = st * jnp.exp(b_end) + jnp.where(bd, u, 0.0)
    return o, st_new


def _hgrn_kernel(qf_ref, vf_ref, kf_ref, gf_ref, qb_ref, vb_ref, kb_ref, gb_ref, s0_ref,
                 of_ref, ob_ref, sfin_ref, st_scr, b_scr, *, batch):
    c = pl.program_id(0)

    @pl.when(c == 0)
    def _():
        st_scr[...] = s0_ref[...]

    def body(bi, carry):
        o, st = _hg_chunk(qf_ref[bi], kf_ref[bi], vf_ref[bi], gf_ref[bi], st_scr[0, bi], b_scr,
                          lambda r: kf_ref[bi, r:r + 1, :], lambda r: vf_ref[bi, r:r + 1, :], False)
        of_ref[bi] = o
        st_scr[0, bi] = st
        o, st = _hg_chunk(qb_ref[bi], kb_ref[bi], vb_ref[bi], gb_ref[bi], st_scr[1, bi], b_scr,
                          lambda r: kb_ref[bi, r:r + 1, :], lambda r: vb_ref[bi, r:r + 1, :], True)
        ob_ref[bi] = o
        st_scr[1, bi] = st
        return carry

    lax.fori_loop(0, batch, body, 0)

    @pl.when(c == pl.num_programs(0) - 1)
    def _():
        sfin_ref[...] = st_scr[...]


def _hgrn_scan(hq, iv, kf, gf, kb, gb, s0, batch, n):
    w = hq.shape[1]
    nc = n // HG_CHUNK
    r3 = lambda a: a.reshape(batch, n, w)
    fwd = pl.BlockSpec((batch, HG_CHUNK, w), lambda c: (0, c, 0))
    bwd = pl.BlockSpec((batch, HG_CHUNK, w), lambda c: (0, nc - 1 - c, 0))
    st_spec = pl.BlockSpec((2, batch, w, w), lambda c: (0, 0, 0, 0))
    o_f, o_b, s_fin = pl.pallas_call(
        functools.partial(_hgrn_kernel, batch=batch),
        grid=(nc,),
        in_specs=[fwd, fwd, fwd, fwd, bwd, bwd, bwd, bwd, st_spec],
        out_specs=[fwd, bwd, st_spec],
        out_shape=[jax.ShapeDtypeStruct((batch, n, w), F32), jax.ShapeDtypeStruct((batch, n, w), F32),
                   jax.ShapeDtypeStruct((2, batch, w, w), F32)],
        scratch_shapes=[pltpu.VMEM((2, batch, w, w), F32), pltpu.VMEM((HG_CHUNK, w), F32)],
        compiler_params=_cparams(("arbitrary",)),
        name="hgrn2_scan",
    )(r3(hq), r3(iv), r3(kf), r3(gf), r3(hq), r3(iv), r3(kb), r3(gb), s0)
    return o_f.reshape(batch * n, w), o_b.reshape(batch * n, w), s_fin


def _tail_kernel(x_ref, ya_ref, yb_ref, of_ref, ob_ref, sg_ref, hnw_ref, wo_ref, g1_ref, nw2_ref, sh2_ref,
                 sc2_ref, g2_ref, w1_ref, w2_ref, fnw_ref, o_ref, xn_scr, h2_scr, acc_scr, *, final, pool_w, att_w):
    j = pl.program_id(1)

    @pl.when(j == 0)
    def _():
        o = of_ref[...] + ob_ref[...]
        w = o.shape[1]
        r_idx = lax.broadcasted_iota(jnp.int32, (w, w), 0) // HEAD_DIM
        c_idx = lax.broadcasted_iota(jnp.int32, (w, w), 1) // HEAD_DIM
        mean_bd = jnp.where(r_idx == c_idx, 1.0 / HEAD_DIM, 0.0).astype(BF16)
        sq = o * o
        sq_hi = sq.astype(BF16)
        sq_lo = (sq - sq_hi.astype(F32)).astype(BF16)
        ms = _dot(sq_hi, mean_bd) + _dot(sq_lo, mean_bd)
        yc = (o * lax.rsqrt(ms + EPS) * hnw_ref[...]) * sg_ref[...]
        y = (_dot(ya_ref[...], wo_ref[0:pool_w, :]) + _dot(yb_ref[...], wo_ref[pool_w:pool_w + att_w, :])
             + _dot(yc.astype(BF16), wo_ref[pool_w + att_w:, :]))
        xn = x_ref[...] + g1_ref[0] * y
        xn_scr[...] = xn
        h2 = (_rms(xn) * nw2_ref[...]) * (1.0 + sc2_ref[0]) + sh2_ref[0]
        h2_scr[...] = h2.astype(BF16)
        acc_scr[...] = jnp.zeros_like(acc_scr)

    hid = jnp.maximum(_dot(h2_scr[...], w1_ref[...]), 0.0)
    acc_scr[...] += _dot((hid * hid).astype(BF16), w2_ref[...])

    @pl.when(j == pl.num_programs(1) - 1)
    def _():
        out = xn_scr[...] + g2_ref[0] * acc_scr[...]
        if final:
            out = _rms(out) * fnw_ref[...]
        o_ref[...] = out


def _block_tail(x, ya, yb, o_f, o_b, sg, hnw_t, wo_bf, mod3, mod_row_of_tile, nw2, w1_bf, w2_bf, fnw, final, tm, tf):
    rows, d = x.shape
    pool_w, att_w, hv_w = ya.shape[1], yb.shape[1], sg.shape[1]
    d_ff = w1_bf.shape[1]
    row = lambda w: pl.BlockSpec((tm, w), lambda i, j: (i, 0))
    vec = lambda w: pl.BlockSpec((1, w), lambda i, j: (0, 0))
    mod = lambda k: pl.BlockSpec((1, 1, d), lambda i, j: (mod_row_of_tile(i) * 6 + k, 0, 0))
    return pl.pallas_call(
        functools.partial(_tail_kernel, final=final, pool_w=pool_w, att_w=att_w),
        grid=(rows // tm, d_ff // tf),
        in_specs=[row(d), row(pool_w), row(att_w), row(hv_w), row(hv_w), row(hv_w), vec(hv_w),
                  pl.BlockSpec((d, d), lambda i, j: (0, 0)), mod(2), vec(d), mod(3), mod(4), mod(5),
                  pl.BlockSpec((d, tf), lambda i, j: (0, j)), pl.BlockSpec((tf, d), lambda i, j: (j, 0)), vec(d)],
        out_specs=row(d),
        out_shape=jax.ShapeDtypeStruct((rows, d), F32),
        scratch_shapes=[pltpu.VMEM((tm, d), F32), pltpu.VMEM((tm, d), BF16), pltpu.VMEM((tm, d), F32)],
        compiler_params=_cparams(("arbitrary", "arbitrary")),
        name="block_tail",
    )(x, ya, yb, o_f, o_b, sg, hnw_t, wo_bf, mod3, nw2.reshape(1, d), mod3, mod3, mod3, w1_bf, w2_bf,
      fnw.reshape(1, d))


def _rope_tables(n):
    pos = np.arange(n)
    row = (pos // GRID_W).astype(np.float32)
    col = (pos % GRID_W).astype(np.float32)
    axis_dims = HEAD_DIM // 2
    inv = jnp.asarray(ROPE_BASE, F32) ** (-jnp.arange(0, axis_dims, 2, dtype=F32) / axis_dims)
    ar = jnp.asarray(row)[:, None] * inv
    ac = jnp.asarray(col)[:, None] * inv
    cos = jnp.concatenate([jnp.cos(ar), jnp.cos(ar), jnp.cos(ac), jnp.cos(ac)], axis=1)
    sin = jnp.concatenate([-jnp.sin(ar), jnp.sin(ar), -jnp.sin(ac), jnp.sin(ac)], axis=1)
    reps = V7X_LANES // HEAD_DIM
    return jnp.tile(cos, (1, reps)), jnp.tile(sin, (1, reps))


def _block_diag(pw):
    g, a, b = pw.shape
    out = jnp.zeros((g * a, g * b), pw.dtype)
    for i in range(g):
        out = out.at[i * a:(i + 1) * a, i * b:(i + 1) * b].set(pw[i])
    return out


def kernel(x, c, ctx, c_ctx, w_ada, b_ada, norm1_w, w_in, pool_w, pool_scale, attn_sink, hg_lower, hg_norm_w, w_out,
           norm2_w, w_mlp1, w_mlp2, final_norm_w):
    batch, seq, d = x.shape
    ctx_len = ctx.shape[1]
    depth = w_ada.shape[0]
    pool_width = pool_scale.shape[1]
    heads = attn_sink.shape[1]
    att_w = heads * HEAD_DIM
    kv_w = (heads // ATT_GROUP) * HEAD_DIM
    hk_w = hg_lower.shape[2]
    hv_w = d - pool_width - att_w
    sizes = (pool_width, att_w, kv_w, hk_w, hv_w)
    assert w_in.shape[2] == pool_width + att_w + 2 * kv_w + 3 * hk_w + 2 * hv_w
    assert hk_w == hv_w and seq % ATT_BLOCK == 0 and seq % HG_CHUNK == 0 and ctx_len % HG_CHUNK == 0

    mod_rows = 16
    cvec = jnp.zeros((mod_rows, d), F32).at[:batch].set(c).at[batch].set(c_ctx)
    mods = _modulation(cvec, w_ada, b_ada)
    rope_tabs = _rope_tables(seq)

    tm_lat, tm_ctx, tf = min(512, seq), ctx_len, 1024
    lat_tiles_per_seq = seq // tm_lat
    lat_row = lambda i: i // lat_tiles_per_seq
    ctx_row = lambda i: batch

    xl = x.reshape(batch * seq, d)
    xc = ctx.reshape(batch * ctx_len, d)
    s_zero = jnp.zeros((2, batch, hv_w, hk_w), F32)
    for l in range(depth):
        need_ctx = l < depth - 1
        final = l == depth - 1
        mod3 = mods[l].reshape(mod_rows * 6, 1, d)
        w_in_bf = w_in[l].astype(BF16)
        wo_bf = w_out[l].astype(BF16)
        w1_bf = w_mlp1[l].astype(BF16)
        w2_bf = w_mlp2[l].astype(BF16)
        pw_bd = _block_diag(pool_w[l]).astype(BF16)
        sink_b = jnp.broadcast_to(attn_sink[l][:, None], (heads, V7X_LANES))
        hnw_t = jnp.tile(hg_norm_w[l], hv_w // HEAD_DIM).reshape(1, hv_w)

        pa, qa, ka, va, hq, kf, gf, kb, gb, iv, sg = _inproj(
            xl, mod3, lat_row, norm1_w[l], w_in_bf, hg_lower, rope_tabs, sizes, l, tm_lat)
        cpa, cqa, cka, cva, chq, ckf, cgf, ckb, cgb, civ, csg = _inproj(
            xc, mod3, ctx_row, norm1_w[l], w_in_bf, hg_lower, None, sizes, l, tm_ctx)

        ya = _pool_mixer(pa, pw_bd, pool_scale[l], seq)
        yb = _window_attention(qa, ka, va, cka, cva, sink_b, batch, seq, ctx_len)
        oc_f, oc_b, s_ctx = _hgrn_scan(chq, civ, ckf, cgf, ckb, cgb, s_zero, batch, ctx_len)
        o_f, o_b, _ = _hgrn_scan(hq, iv, kf, gf, kb, gb, s_ctx, batch, seq)
        xl_new = _block_tail(xl, ya, yb, o_f, o_b, sg, hnw_t, wo_bf, mod3, lat_row, norm2_w[l], w1_bf, w2_bf,
                             final_norm_w, final, tm_lat, tf)
        if need_ctx:
            yca = _pool_mixer(cpa, pw_bd, pool_scale[l], ctx_len)
            ycb = _context_attention(cqa, cka, cva, sink_b, batch, ctx_len)
            xc = _block_tail(xc, yca, ycb, oc_f, oc_b, csg, hnw_t, wo_bf, mod3, ctx_row, norm2_w[l], w1_bf, w2_bf,
                             final_norm_w, False, tm_ctx, tf)
        xl = xl_new
    return xl.reshape(batch, seq, d)
```
